```python
import math
import jax, jax.numpy as jnp
from jax import lax
import numpy as np

D_MODEL = 2048
BATCH = 2
SEQ = 16384
DEPTH = 1

HEAD_DIM = 128
ATTN_HEADS = 8
GMLP_GROUPS = 8
ATTN_WIDTH = ATTN_HEADS * HEAD_DIM
GMLP_WIDTH = GMLP_GROUPS * HEAD_DIM
MIX_WIDTH = ATTN_WIDTH + GMLP_WIDTH
IN_PROJ_WIDTH = 3 * ATTN_WIDTH + 2 * GMLP_WIDTH
DILATION_BRANCHES = ((128, 1), (512, 4), (2048, 16))
BAND_BLOCK = 128
GMLP_CHUNK = 128
ROPE_THETA = 10000.0
MEM_LEN = 256
XATTN_HEADS = 4
XATTN_HEAD_DIM = D_MODEL // XATTN_HEADS
D_FF = -(-(8 * D_MODEL) // (3 * 256)) * 256
DEEPNORM_ALPHA = (2 * DEPTH) ** 0.25
DEEPNORM_BETA = (8 * DEPTH) ** -0.25
LN_EPS = 1e-5

kernel_name = "hymba_dilated_gmlp_deepnorm_layer"


def _layer_norm(x, g, b):
    xf = x.astype(jnp.float32)
    mu = jnp.mean(xf, axis=-1, keepdims=True)
    var = jnp.mean(jnp.square(xf - mu), axis=-1, keepdims=True)
    y = (xf - mu) * lax.rsqrt(var + LN_EPS) * g.astype(jnp.float32) + b.astype(jnp.float32)
    return y.astype(x.dtype)


def _rms_norm(x, g):
    xf = x.astype(jnp.float32)
    y = xf * lax.rsqrt(jnp.mean(jnp.square(xf), axis=-1, keepdims=True) + LN_EPS) * g.astype(jnp.float32)
    return y.astype(x.dtype)


def _rope(x, positions):
    half = x.shape[-1] // 2
    inv_freq = ROPE_THETA ** (-jnp.arange(half, dtype=jnp.float32) / half)
    ang = positions.astype(jnp.float32)[:, :, None] * inv_freq
    cos = jnp.cos(ang)[:, :, None, :]
    sin = jnp.sin(ang)[:, :, None, :]
    xf = x.astype(jnp.float32)
    x1, x2 = xf[..., :half], xf[..., half:]
    return jnp.concatenate([x1 * cos - x2 * sin, x2 * cos + x1 * sin], axis=-1).astype(x.dtype)


def _banded_causal_attention(q, k, v, reach):
    N, L, H, E = q.shape
    nb = -(-L // BAND_BLOCK)
    Lp = nb * BAND_BLOCK
    pad = ((0, 0), (0, Lp - L), (0, 0), (0, 0))
    qb = jnp.pad(q, pad).reshape(N, nb, BAND_BLOCK, H, E)
    kb = jnp.pad(k, pad).reshape(N, nb, BAND_BLOCK, H, E)
    vb = jnp.pad(v, pad).reshape(N, nb, BAND_BLOCK, H, E)

    def with_prev(t):
        prev = jnp.pad(t[:, :-1], ((0, 0), (1, 0), (0, 0), (0, 0), (0, 0)))
        return jnp.concatenate([prev, t], axis=2)

    kw, vw = with_prev(kb), with_prev(vb)
    s = jnp.einsum('nbqhe,nbkhe->nbhqk', qb, kw).astype(jnp.float32) * (E ** -0.5)
    qpos = jnp.arange(BAND_BLOCK)[:, None] + BAND_BLOCK
    kpos = jnp.arange(2 * BAND_BLOCK)[None, :]
    dist = qpos - kpos
    band = (dist >= 0) & (dist <= reach)
    prev_ok = (jnp.arange(nb)[:, None, None] > 0) | (kpos[None] >= BAND_BLOCK)
    mask = band[None] & prev_ok
    s = jnp.where(mask[None, :, None], s, -jnp.inf)
    lse = jax.nn.logsumexp(s, axis=-1)
    p = jnp.exp(s - lse[..., None])
    o = jnp.einsum('nbhqk,nbkhe->nbqhe', p.astype(v.dtype), vw)
    o = o.reshape(N, Lp, H, E)[:, :L]
    lse = lse.transpose(0, 1, 3, 2).reshape(N, Lp, H)[:, :L]
    return o, lse


def _dilated_attention(q, k, v):
    B, S, H, E = q.shape
    outs, lses = [], []
    for window, dil in DILATION_BRANCHES:
        reach = window // dil
        Ld = S // dil

        def to_strided(t):
            return t.reshape(B, Ld, dil, H, E).transpose(0, 2, 1, 3, 4).reshape(B * dil, Ld, H, E)

        o, lse = _banded_causal_attention(to_strided(q), to_strided(k), to_strided(v), reach)
        outs.append(o.reshape(B, dil, Ld, H, E).transpose(0, 2, 1, 3, 4).reshape(B, S, H, E))
        lses.append(lse.reshape(B, dil, Ld, H).transpose(0, 2, 1, 3).reshape(B, S, H))
    wts = jax.nn.softmax(jnp.stack(lses, axis=0), axis=0)
    out = jnp.einsum('rbsh,rbshe->bshe', wts, jnp.stack(outs, axis=0).astype(jnp.float32))
    return out.astype(q.dtype)


def _spatial_gating(u, g, norm_g, norm_b, w_s, b_s):
    B, S, _ = u.shape
    u = jax.nn.gelu(u, approximate=False)
    g = _layer_norm(jax.nn.gelu(g, approximate=False), norm_g, norm_b)
    gc = g.reshape(B, S // GMLP_CHUNK, GMLP_CHUNK, GMLP_GROUPS, HEAD_DIM)
    w = jnp.tril(w_s).astype(g.dtype)
    mixed = jnp.einsum('gij,bcjge->bcige', w, gc) + b_s.T[None, None, :, :, None]
    return u * mixed.reshape(B, S, GMLP_WIDTH)


def _memory_cross_attention(h, mem, w_q, w_k, w_v, w_o):
    B, S, D = h.shape
    M = mem.shape[1]
    q = (h @ w_q).reshape(B, S, XATTN_HEADS, XATTN_HEAD_DIM)
    k = (mem @ w_k).reshape(B, M, XATTN_HEADS, XATTN_HEAD_DIM)
    v = (mem @ w_v).reshape(B, M, XATTN_HEADS, XATTN_HEAD_DIM)
    s = jnp.einsum('bshe,bmhe->bhsm', q, k).astype(jnp.float32) * (XATTN_HEAD_DIM ** -0.5)
    p = jax.nn.softmax(s, axis=-1)
    o = jnp.einsum('bhsm,bmhe->bshe', p.astype(v.dtype), v).reshape(B, S, D)
    return o @ w_o


def setup_inputs(seed: int = 0) -> dict:
    key = jax.random.key(seed)
    ks = jax.random.split(key, 32)
    f32 = jnp.float32
    L = DEPTH
    D = D_MODEL

    def nrm(k, shape, scale):
        return jax.random.normal(k, shape, f32) * scale

    def gain(k, shape):
        return 1.0 + 0.02 * jax.random.normal(k, shape, f32)

    def bias(k, shape):
        return 0.02 * jax.random.normal(k, shape, f32)

    x = jax.random.normal(ks[0], (BATCH, SEQ, D), f32)
    mem = jax.random.normal(ks[1], (BATCH, MEM_LEN, D), f32)
    start = jax.random.randint(ks[2], (BATCH, 1), 0, 4096, dtype=jnp.int32)
    positions = start + jnp.arange(SEQ, dtype=jnp.int32)[None, :]

    col_scale = jnp.concatenate([jnp.ones((2 * ATTN_WIDTH,), f32),
                                 jnp.full((ATTN_WIDTH + 2 * GMLP_WIDTH,), DEEPNORM_BETA, f32)])
    w_in = nrm(ks[5], (L, D, IN_PROJ_WIDTH), D ** -0.5) * col_scale

    return {
        "x": x,
        "mem": mem,
        "positions": positions,
        "ln_in_g": gain(ks[3], (D,)),
        "ln_in_b": bias(ks[4], (D,)),
        "w_in": w_in,
        "sgu_norm_g": gain(ks[6], (L, GMLP_WIDTH)),
        "sgu_norm_b": bias(ks[7], (L, GMLP_WIDTH)),
        "w_spatial": nrm(ks[8], (L, GMLP_GROUPS, GMLP_CHUNK, GMLP_CHUNK), GMLP_CHUNK ** -0.5),
        "b_spatial": 1.0 + 0.1 * jax.random.normal(ks[9], (L, GMLP_GROUPS, GMLP_CHUNK), f32),
        "attn_out_g": gain(ks[10], (L, ATTN_WIDTH)),
        "gmlp_out_g": gain(ks[11], (L, GMLP_WIDTH)),
        "w_mix_out": nrm(ks[12], (L, MIX_WIDTH, D), DEEPNORM_BETA * MIX_WIDTH ** -0.5),
        "ln1_g": gain(ks[13], (L, D)),
        "ln1_b": bias(ks[14], (L, D)),
        "w_xq": nrm(ks[15], (L, D, D), D ** -0.5),
        "w_xk": nrm(ks[16], (L, D, D), D ** -0.5),
        "w_xv": nrm(ks[17], (L, D, D), DEEPNORM_BETA * D ** -0.5),
        "w_xo": nrm(ks[18], (L, D, D), DEEPNORM_BETA * D ** -0.5),
        "ln2_g": gain(ks[19], (L, D)),
        "ln2_b": bias(ks[20], (L, D)),
        "w_ffn_gate": nrm(ks[21], (L, D, D_FF), DEEPNORM_BETA * D ** -0.5),
        "w_ffn_up": nrm(ks[22], (L, D, D_FF), DEEPNORM_BETA * D ** -0.5),
        "w_ffn_down": nrm(ks[23], (L, D_FF, D), DEEPNORM_BETA * D_FF ** -0.5),
        "ln3_g": gain(ks[24], (L, D)),
        "ln3_b": bias(ks[25], (L, D)),
    }


def reference(x, mem, positions, ln_in_g, ln_in_b, w_in, sgu_norm_g, sgu_norm_b,
              w_spatial, b_spatial, attn_out_g, gmlp_out_g, w_mix_out, ln1_g, ln1_b,
              w_xq, w_xk, w_xv, w_xo, ln2_g, ln2_b, w_ffn_gate, w_ffn_up, w_ffn_down,
              ln3_g, ln3_b):
    B, S, D = x.shape
    h = _layer_norm(x, ln_in_g, ln_in_b)
    for l in range(DEPTH):
        proj = h @ w_in[l]
        q, k, v, u, g = jnp.split(
            proj, [ATTN_WIDTH, 2 * ATTN_WIDTH, 3 * ATTN_WIDTH, 3 * ATTN_WIDTH + GMLP_WIDTH], axis=-1)
        q = _rope(q.reshape(B, S, ATTN_HEADS, HEAD_DIM), positions)
        k = _rope(k.reshape(B, S, ATTN_HEADS, HEAD_DIM), positions)
        v = v.reshape(B, S, ATTN_HEADS, HEAD_DIM)
        attn = _dilated_attention(q, k, v).reshape(B, S, ATTN_WIDTH)
        gm = _spatial_gating(u, g, sgu_norm_g[l], sgu_norm_b[l], w_spatial[l], b_spatial[l])
        mixed = jnp.concatenate([_rms_norm(attn, attn_out_g[l]),
                                 _rms_norm(gm, gmlp_out_g[l])], axis=-1) @ w_mix_out[l]
        h = _layer_norm(DEEPNORM_ALPHA * h + mixed, ln1_g[l], ln1_b[l])
        xa = _memory_cross_attention(h, mem, w_xq[l], w_xk[l], w_xv[l], w_xo[l])
        h = _layer_norm(DEEPNORM_ALPHA * h + xa, ln2_g[l], ln2_b[l])
        ff = (jax.nn.silu(h @ w_ffn_gate[l]) * (h @ w_ffn_up[l])) @ w_ffn_down[l]
        h = _layer_norm(DEEPNORM_ALPHA * h + ff, ln3_g[l], ln3_b[l])
    return h
```

```python
import functools
import math

import jax
import jax.numpy as jnp
from jax import lax
from jax.experimental import pallas as pl
from jax.experimental.pallas import tpu as pltpu

F32 = jnp.float32
BF16 = jnp.bfloat16

D_MODEL = 2048
HEAD_DIM = 128
HALF = HEAD_DIM // 2
ATTN_HEADS = 8
GMLP_GROUPS = 8
ATTN_WIDTH = ATTN_HEADS * HEAD_DIM
GMLP_WIDTH = GMLP_GROUPS * HEAD_DIM
GMLP_CHUNK = 128
BAND_BLOCK = 128
ROPE_THETA = 10000.0
MEM_LEN = 256
XATTN_HEADS = 4
XATTN_HEAD_DIM = D_MODEL // XATTN_HEADS
DEPTH = 1
DEEPNORM_ALPHA = (2 * DEPTH) ** 0.25
LN_EPS = 1e-5
NEG_BIG = -1e30

PERM_TILE = 512
PERM_R = 16
PERM_I = PERM_TILE // PERM_R

VMEM_LIMIT = 56 * 1024 * 1024


def _layer_norm(x, g, b):
    mu = jnp.mean(x, axis=-1, keepdims=True)
    xc = x - mu
    var = jnp.mean(xc * xc, axis=-1, keepdims=True)
    return xc * lax.rsqrt(var + LN_EPS) * g + b


def _rms_norm(x, g):
    ms = jnp.mean(x * x, axis=-1, keepdims=True)
    return x * lax.rsqrt(ms + LN_EPS) * g


def _gelu(x):
    return 0.5 * x * (1.0 + lax.erf(x * (1.0 / math.sqrt(2.0))))


def _to_strided(src_ref, dst_ref, col0):
    rows = src_ref.shape[0]
    for t in range(rows // PERM_TILE):
        base = t * PERM_TILE
        for r in range(PERM_R):
            blk = src_ref[pl.ds(base + r, PERM_I, stride=PERM_R), :]
            dst_ref[base + r * PERM_I: base + (r + 1) * PERM_I, col0:col0 + HEAD_DIM] = blk.astype(dst_ref.dtype)


def _from_strided(src_ref):
    rows = src_ref.shape[0]
    parts = []
    for t in range(rows // PERM_TILE):
        base = t * PERM_TILE
        for i in range(PERM_I):
            parts.append(src_ref[pl.ds(base + i, PERM_R, stride=PERM_I), :])
    return jnp.concatenate(parts, axis=0)


def _in_proj_kernel(x_ref, pos_ref, invf_ref, lng_ref, lnb_ref, w_ref, sng_ref, snb_ref,
                    ws_ref, bs_ref, gog_ref,
                    nat_ref, str_ref, gm_ref,
                    hn_s, cos_s, sin_s, y_s, u_s):
    j = pl.program_id(1)
    tm = x_ref.shape[0]

    @pl.when(j == 0)
    def _prologue():
        hn_s[...] = _layer_norm(x_ref[...], lng_ref[...], lnb_ref[...]).astype(BF16)
        ang = pos_ref[...].astype(F32) * invf_ref[...]
        lane = lax.broadcasted_iota(jnp.int32, ang.shape, 1)
        sn = jnp.sin(ang)
        cos_s[...] = jnp.cos(ang)
        sin_s[...] = jnp.where(lane < HALF, -sn, sn)

    y = jnp.dot(hn_s[...], w_ref[...], preferred_element_type=F32)

    @pl.when(j < 2)
    def _rope():
        c = cos_s[...]
        s = sin_s[...]
        for h in range(ATTN_HEADS):
            sl = slice(h * HEAD_DIM, (h + 1) * HEAD_DIM)
            xh = y[:, sl]
            y_s[h] = xh * c + pltpu.roll(xh, HALF, axis=1) * s
            nat_ref[:, sl] = y_s[h].astype(BF16)
            _to_strided(y_s.at[h], str_ref, h * HEAD_DIM)

    @pl.when(j == 2)
    def _value():
        nat_ref[...] = y.astype(BF16)
        for h in range(ATTN_HEADS):
            y_s[h] = y[:, h * HEAD_DIM:(h + 1) * HEAD_DIM]
            _to_strided(y_s.at[h], str_ref, h * HEAD_DIM)

    @pl.when(j == 3)
    def _gate_u():
        u_s[...] = _gelu(y)

    @pl.when(j == 4)
    def _spatial_gate():
        g = _layer_norm(_gelu(y), sng_ref[...], snb_ref[...]).astype(BF16)
        row = lax.broadcasted_iota(jnp.int32, (GMLP_CHUNK, GMLP_CHUNK), 0)
        col = lax.broadcasted_iota(jnp.int32, (GMLP_CHUNK, GMLP_CHUNK), 1)
        causal = row >= col
        for gr in range(GMLP_GROUPS):
            sl = slice(gr * HEAD_DIM, (gr + 1) * HEAD_DIM)
            w = jnp.where(causal, ws_ref[gr], 0.0).astype(BF16)
            bcol = bs_ref[:, gr:gr + 1]
            for c in range(tm // GMLP_CHUNK):
                rs = slice(c * GMLP_CHUNK, (c + 1) * GMLP_CHUNK)
                mixed = jnp.dot(w, g[rs, sl], preferred_element_type=F32) + bcol
                u_s[rs, sl] = u_s[rs, sl] * mixed
        gm_ref[...] = _rms_norm(u_s[...], gog_ref[...]).astype(BF16)


def _in_proj(x2, pos2, invf, ln_g, ln_b, w_in, sgu_g, sgu_b, w_sp, b_sp_t, gm_g, tm):
    T = x2.shape[0]
    nseg = w_in.shape[1] // ATTN_WIDTH
    const = lambda i, j: (0, 0)
    return pl.pallas_call(
        _in_proj_kernel,
        grid=(T // tm, nseg),
        in_specs=[
            pl.BlockSpec((tm, D_MODEL), lambda i, j: (i, 0)),
            pl.BlockSpec((tm, 1), lambda i, j: (i, 0)),
            pl.BlockSpec((1, HEAD_DIM), const),
            pl.BlockSpec((1, D_MODEL), const),
            pl.BlockSpec((1, D_MODEL), const),
            pl.BlockSpec((D_MODEL, ATTN_WIDTH), lambda i, j: (0, j)),
            pl.BlockSpec((1, GMLP_WIDTH), const),
            pl.BlockSpec((1, GMLP_WIDTH), const),
            pl.BlockSpec((GMLP_GROUPS, GMLP_CHUNK, GMLP_CHUNK), lambda i, j: (0, 0, 0)),
            pl.BlockSpec((GMLP_CHUNK, GMLP_GROUPS), const),
            pl.BlockSpec((1, GMLP_WIDTH), const),
        ],
        out_specs=[
            pl.BlockSpec((tm, ATTN_WIDTH), lambda i, j: (i, jnp.minimum(j, 2))),
            pl.BlockSpec((tm, ATTN_WIDTH), lambda i, j: (i, jnp.minimum(j, 2))),
            pl.BlockSpec((tm, GMLP_WIDTH), lambda i, j: (i, 0)),
        ],
        out_shape=[
            jax.ShapeDtypeStruct((T, 3 * ATTN_WIDTH), BF16),
            jax.ShapeDtypeStruct((T, 3 * ATTN_WIDTH), BF16),
            jax.ShapeDtypeStruct((T, GMLP_WIDTH), BF16),
        ],
        scratch_shapes=[
            pltpu.VMEM((tm, D_MODEL), BF16),
            pltpu.VMEM((tm, HEAD_DIM), F32),
            pltpu.VMEM((tm, HEAD_DIM), F32),
            pltpu.VMEM((ATTN_HEADS, tm, HEAD_DIM), F32),
            pltpu.VMEM((tm, GMLP_WIDTH), F32),
        ],
        compiler_params=pltpu.CompilerParams(
            dimension_semantics=("arbitrary", "arbitrary"), vmem_limit_bytes=VMEM_LIMIT),
        name="in_proj",
    )(x2, pos2, invf, ln_g, ln_b, w_in, sgu_g, sgu_b, w_sp, b_sp_t, gm_g)


def _band_attn_kernel(q_ref, kp_ref, kc_ref, vp_ref, vc_ref, o_ref, lse_ref, *, interleave, head_major):
    has_prev = pl.program_id(1) > 0
    blk = (BAND_BLOCK, ATTN_WIDTH)
    q = q_ref[...].reshape(blk)
    kp = kp_ref[...].reshape(blk)
    kc = kc_ref[...].reshape(blk)
    vp = vp_ref[...].reshape(blk)
    vc = vc_ref[...].reshape(blk)

    a_q = lax.broadcasted_iota(jnp.int32, (BAND_BLOCK, BAND_BLOCK), 0)
    a_k = lax.broadcasted_iota(jnp.int32, (BAND_BLOCK, BAND_BLOCK), 1)
    if interleave:
        a_q = 4 * (a_q % PERM_I) + a_q // PERM_I
        a_k = 4 * (a_k % PERM_I) + a_k // PERM_I
    mask_c = a_k <= a_q
    mask_p = jnp.logical_and(a_k >= a_q, has_prev)

    scale = HEAD_DIM ** -0.5
    nt = (((1,), (1,)), ((), ()))
    lane = lax.broadcasted_iota(jnp.int32, (BAND_BLOCK, HEAD_DIM), 1)
    lse_all = jnp.zeros((BAND_BLOCK, HEAD_DIM), F32)
    for h in range(ATTN_HEADS):
        sl = slice(h * HEAD_DIM, (h + 1) * HEAD_DIM)
        qh = q[:, sl]
        s_c = lax.dot_general(qh, kc[:, sl], nt, preferred_element_type=F32) * scale
        s_p = lax.dot_general(qh, kp[:, sl], nt, preferred_element_type=F32) * scale
        s_c = jnp.where(mask_c, s_c, NEG_BIG)
        s_p = jnp.where(mask_p, s_p, NEG_BIG)
        m = jnp.maximum(jnp.max(s_c, axis=1, keepdims=True), jnp.max(s_p, axis=1, keepdims=True))
        p_c = jnp.exp(s_c - m)
        p_p = jnp.exp(s_p - m)
        l = jnp.sum(p_c, axis=1, keepdims=True) + jnp.sum(p_p, axis=1, keepdims=True)
        acc = jnp.dot(p_c.astype(BF16), vc[:, sl], preferred_element_type=F32)
        acc = acc + jnp.dot(p_p.astype(BF16), vp[:, sl], preferred_element_type=F32)
        out_h = acc / l
        if head_major:
            o_ref[h] = out_h.reshape(o_ref.shape[1:])
        else:
            o_ref[:, sl] = out_h.astype(o_ref.dtype)
        lse_all = jnp.where(lane == h, m + jnp.log(l), lse_all)
    lse_ref[...] = lse_all.reshape(lse_ref.shape)


def _band_attention(qkv, batch, seq, dil):
    T = qkv.shape[0]
    params = pltpu.CompilerParams(dimension_semantics=("arbitrary",) * 3, vmem_limit_bytes=VMEM_LIMIT)
    if dil == 1:
        nb = seq // BAND_BLOCK
        cur = lambda c: (lambda b, jb, z: (b * nb + jb, c))
        prev = lambda c: (lambda b, jb, z: (b * nb + jnp.maximum(jb - 1, 0), c))
        blk = (BAND_BLOCK, ATTN_WIDTH)
        return pl.pallas_call(
            functools.partial(_band_attn_kernel, interleave=False, head_major=False),
            grid=(batch, nb, 1),
            in_specs=[pl.BlockSpec(blk, cur(0)), pl.BlockSpec(blk, prev(1)), pl.BlockSpec(blk, cur(1)),
                      pl.BlockSpec(blk, prev(2)), pl.BlockSpec(blk, cur(2))],
            out_specs=[pl.BlockSpec(blk, cur(0)), pl.BlockSpec((BAND_BLOCK, HEAD_DIM), cur(0))],
            out_shape=[jax.ShapeDtypeStruct((T, ATTN_WIDTH), BF16),
                       jax.ShapeDtypeStruct((T, HEAD_DIM), F32)],
            compiler_params=params, name="attn_d1",
        )(qkv, qkv, qkv, qkv, qkv)
    ntile = seq // PERM_TILE
    if dil == 4:
        view = (batch * ntile, 4, 4, PERM_I)
        cur = lambda b, t, r0: (b * ntile + t, 0, r0, 0)
        prev = lambda b, t, r0: (b * ntile + jnp.maximum(t - 1, 0), 0, r0, 0)
        blk = (None, 4, None, PERM_I)
        grid = (batch, ntile, 4)
    else:
        nsup = ntile // 4
        view = (batch * ntile, PERM_R, PERM_I)
        cur = lambda b, t, r: (b * nsup + t, r, 0)
        prev = lambda b, t, r: (b * nsup + jnp.maximum(t - 1, 0), r, 0)
        blk = (4, None, PERM_I)
        grid = (batch, nsup, PERM_R)
    col = lambda index, c: (lambda *g: index(*g) + (c,))
    qv = qkv.reshape(view + (3 * ATTN_WIDTH,))
    bq = blk + (ATTN_WIDTH,)
    o, lse = pl.pallas_call(
        functools.partial(_band_attn_kernel, interleave=(dil == 4), head_major=True),
        grid=grid,
        in_specs=[pl.BlockSpec(bq, col(cur, 0)), pl.BlockSpec(bq, col(prev, 1)), pl.BlockSpec(bq, col(cur, 1)),
                  pl.BlockSpec(bq, col(prev, 2)), pl.BlockSpec(bq, col(cur, 2))],
        out_specs=[pl.BlockSpec((ATTN_HEADS,) + blk + (HEAD_DIM,), lambda *g: (0,) + cur(*g) + (0,)),
                   pl.BlockSpec(blk + (HEAD_DIM,), col(cur, 0))],
        out_shape=[jax.ShapeDtypeStruct((ATTN_HEADS,) + view + (HEAD_DIM,), F32),
                   jax.ShapeDtypeStruct(view + (HEAD_DIM,), F32)],
        compiler_params=params, name="attn_d%d" % dil,
    )(qv, qv, qv, qv, qv)
    return o.reshape(ATTN_HEADS, T, HEAD_DIM), lse.reshape(T, HEAD_DIM)


def _mix_out_kernel(x_ref, o1_ref, o4_ref, o16_ref, l1_ref, l4_ref, l16_ref, gm_ref,
                    lng_ref, lnb_ref, aog_ref, w_ref, g1_ref, b1_ref, h_ref, attn_s):
    l1 = l1_ref[...]
    l4 = _from_strided(l4_ref)
    l16 = _from_strided(l16_ref)
    m = jnp.maximum(jnp.maximum(l1, l4), l16)
    e1 = jnp.exp(l1 - m)
    e4 = jnp.exp(l4 - m)
    e16 = jnp.exp(l16 - m)
    den = e1 + e4 + e16
    w1, w4, w16 = e1 / den, e4 / den, e16 / den
    for h in range(ATTN_HEADS):
        sl = slice(h * HEAD_DIM, (h + 1) * HEAD_DIM)
        attn_s[:, sl] = (w1[:, h:h + 1] * o1_ref[:, sl].astype(F32)
                         + w4[:, h:h + 1] * _from_strided(o4_ref.at[h])
                         + w16[:, h:h + 1] * _from_strided(o16_ref.at[h]))
    attn_n = _rms_norm(attn_s[...], aog_ref[...]).astype(BF16)
    mixed = jnp.dot(attn_n, w_ref[:ATTN_WIDTH, :], preferred_element_type=F32)
    mixed = mixed + jnp.dot(gm_ref[...], w_ref[ATTN_WIDTH:, :], preferred_element_type=F32)
    h0 = _layer_norm(x_ref[...], lng_ref[...], lnb_ref[...])
    h_ref[...] = _layer_norm(DEEPNORM_ALPHA * h0 + mixed, g1_ref[...], b1_ref[...])


def _mix_out(x2, o1, o4, o16, l1, l4, l16, gm_n, ln_g, ln_b, ao_g, w_mix, g1, b1, tm):
    T = x2.shape[0]
    row = lambda i: (i, 0)
    hrow = lambda i: (0, i, 0)
    const = lambda i: (0, 0)
    return pl.pallas_call(
        _mix_out_kernel,
        grid=(T // tm,),
        in_specs=[
            pl.BlockSpec((tm, D_MODEL), row),
            pl.BlockSpec((tm, ATTN_WIDTH), row),
            pl.BlockSpec((ATTN_HEADS, tm, HEAD_DIM), hrow),
            pl.BlockSpec((ATTN_HEADS, tm, HEAD_DIM), hrow),
            pl.BlockSpec((tm, HEAD_DIM), row),
            pl.BlockSpec((tm, HEAD_DIM), row),
            pl.BlockSpec((tm, HEAD_DIM), row),
            pl.BlockSpec((tm, GMLP_WIDTH), row),
            pl.BlockSpec((1, D_MODEL), const),
            pl.BlockSpec((1, D_MODEL), const),
            pl.BlockSpec((1, ATTN_WIDTH), const),
            pl.BlockSpec((ATTN_WIDTH + GMLP_WIDTH, D_MODEL), const),
            pl.BlockSpec((1, D_MODEL), const),
            pl.BlockSpec((1, D_MODEL), const),
        ],
        out_specs=pl.BlockSpec((tm, D_MODEL), row),
        out_shape=jax.ShapeDtypeStruct((T, D_MODEL), F32),
        scratch_shapes=[pltpu.VMEM((tm, ATTN_WIDTH), F32)],
        compiler_params=pltpu.CompilerParams(
            dimension_semantics=("arbitrary",), vmem_limit_bytes=VMEM_LIMIT),
        name="mix_out",
    )(x2, o1, o4, o16, l1, l4, l16, gm_n, ln_g, ln_b, ao_g, w_mix, g1, b1)


def _matmul_kernel(a_ref, w_ref, o_ref):
    o_ref[...] = jnp.dot(a_ref[...].astype(BF16), w_ref[...],
                         preferred_element_type=F32).astype(o_ref.dtype)


def _matmul(a, w, tn, out_dtype):
    M, K = a.shape
    N = w.shape[1]
    return pl.pallas_call(
        _matmul_kernel,
        grid=(N // tn,),
        in_specs=[pl.BlockSpec((M, K), lambda j: (0, 0)), pl.BlockSpec((K, tn), lambda j: (0, j))],
        out_specs=pl.BlockSpec((M, tn), lambda j: (0, j)),
        out_shape=jax.ShapeDtypeStruct((M, N), out_dtype),
        compiler_params=pltpu.CompilerParams(
            dimension_semantics=("arbitrary",), vmem_limit_bytes=VMEM_LIMIT),
        name="mem_proj",
    )(a, w)


def _xattn_kernel(h_ref, wq_ref, k_ref, v_ref, wo_ref, g_ref, b_ref, out_ref, o_s):
    h = h_ref[...]
    q = jnp.dot(h.astype(BF16), wq_ref[...], preferred_element_type=F32).astype(BF16)
    scale = XATTN_HEAD_DIM ** -0.5
    nt = (((1,), (1,)), ((), ()))
    for hd in range(XATTN_HEADS):
        sl = slice(hd * XATTN_HEAD_DIM, (hd + 1) * XATTN_HEAD_DIM)
        s = lax.dot_general(q[:, sl], k_ref[:, sl], nt, preferred_element_type=F32) * scale
        m = jnp.max(s, axis=1, keepdims=True)
        p = jnp.exp(s - m)
        l = jnp.sum(p, axis=1, keepdims=True)
        o = jnp.dot(p.astype(BF16), v_ref[:, sl], preferred_element_type=F32) / l
        o_s[:, sl] = o.astype(BF16)
    xa = jnp.dot(o_s[...], wo_ref[...], preferred_element_type=F32)
    out_ref[...] = _layer_norm(DEEPNORM_ALPHA * h + xa, g_ref[...], b_ref[...])


def _xattn(h1, wq, kx, vx, wo, g2, b2, seq, tm):
    T = h1.shape[0]
    per_batch = seq // tm
    row = lambda i: (i, 0)
    const = lambda i: (0, 0)
    mem = lambda i: (i // per_batch, 0)
    return pl.pallas_call(
        _xattn_kernel,
        grid=(T // tm,),
        in_specs=[
            pl.BlockSpec((tm, D_MODEL), row),
            pl.BlockSpec((D_MODEL, D_MODEL), const),
            pl.BlockSpec((MEM_LEN, D_MODEL), mem),
            pl.BlockSpec((MEM_LEN, D_MODEL), mem),
            pl.BlockSpec((D_MODEL, D_MODEL), const),
            pl.BlockSpec((1, D_MODEL), const),
            pl.BlockSpec((1, D_MODEL), const),
        ],
        out_specs=pl.BlockSpec((tm, D_MODEL), row),
        out_shape=jax.ShapeDtypeStruct((T, D_MODEL), F32),
        scratch_shapes=[pltpu.VMEM((tm, D_MODEL), BF16)],
        compiler_params=pltpu.CompilerParams(
            dimension_semantics=("arbitrary",), vmem_limit_bytes=VMEM_LIMIT),
        name="xattn",
    )(h1, wq, kx, vx, wo, g2, b2)


def _ffn_kernel(h_ref, wg_ref, wu_ref, wd_ref, g_ref, b_ref, out_ref, hb_s):
    f = pl.program_id(1)

    @pl.when(f == 0)
    def _init():
        hb_s[...] = h_ref[...].astype(BF16)

    hb = hb_s[...]
    gate = jnp.dot(hb, wg_ref[...], preferred_element_type=F32)
    up = jnp.dot(hb, wu_ref[...], preferred_element_type=F32)
    act = (gate * (1.0 / (1.0 + jnp.exp(-gate))) * up).astype(BF16)
    part = jnp.dot(act, wd_ref[...], preferred_element_type=F32)

    @pl.when(f == 0)
    def _first():
        out_ref[...] = part

    @pl.when(f > 0)
    def _accumulate():
        out_ref[...] += part

    @pl.when(f == pl.num_programs(1) - 1)
    def _finish():
        out_ref[...] = _layer_norm(DEEPNORM_ALPHA * h_ref[...] + out_ref[...], g_ref[...], b_ref[...])


def _ffn(h2, wg, wu, wd, g3, b3, tm, tf):
    T = h2.shape[0]
    d_ff = wg.shape[1]
    return pl.pallas_call(
        _ffn_kernel,
        grid=(T // tm, d_ff // tf),
        in_specs=[
            pl.BlockSpec((tm, D_MODEL), lambda i, f: (i, 0)),
            pl.BlockSpec((D_MODEL, tf), lambda i, f: (0, f)),
            pl.BlockSpec((D_MODEL, tf), lambda i, f: (0, f)),
            pl.BlockSpec((tf, D_MODEL), lambda i, f: (f, 0)),
            pl.BlockSpec((1, D_MODEL), lambda i, f: (0, 0)),
            pl.BlockSpec((1, D_MODEL), lambda i, f: (0, 0)),
        ],
        out_specs=pl.BlockSpec((tm, D_MODEL), lambda i, f: (i, 0)),
        out_shape=jax.ShapeDtypeStruct((T, D_MODEL), F32),
        scratch_shapes=[pltpu.VMEM((tm, D_MODEL), BF16)],
        compiler_params=pltpu.CompilerParams(
            dimension_semantics=("arbitrary", "arbitrary"), vmem_limit_bytes=VMEM_LIMIT),
        name="ffn",
    )(h2, wg, wu, wd, g3, b3)


def kernel(x, mem, positions, ln_in_g, ln_in_b, w_in, sgu_norm_g, sgu_norm_b, w_spatial, b_spatial,
           attn_out_g, gmlp_out_g, w_mix_out, ln1_g, ln1_b, w_xq, w_xk, w_xv, w_xo, ln2_g, ln2_b,
           w_ffn_gate, w_ffn_up, w_ffn_down, ln3_g, ln3_b):
    B, S, D = x.shape
    T = B * S
    assert D == D_MODEL and S % (4 * PERM_TILE) == 0 and w_in.shape[0] == DEPTH == 1
    row = lambda v: v.reshape(1, -1)

    x2 = x.reshape(T, D)
    pos2 = positions.reshape(T, 1)
    inv_freq = ROPE_THETA ** (-jnp.arange(HALF, dtype=F32) / HALF)
    invf = jnp.concatenate([inv_freq, inv_freq]).reshape(1, HEAD_DIM)

    nat, strided, gm_n = _in_proj(
        x2, pos2, invf, row(ln_in_g), row(ln_in_b), w_in[0].astype(BF16),
        row(sgu_norm_g[0]), row(sgu_norm_b[0]), w_spatial[0], b_spatial[0].T, row(gmlp_out_g[0]),
        tm=PERM_TILE)

    o1, l1 = _band_attention(nat, B, S, 1)
    o4, l4 = _band_attention(strided, B, S, 4)
    o16, l16 = _band_attention(strided, B, S, 16)

    h1 = _mix_out(x2, o1, o4, o16, l1, l4, l16, gm_n, row(ln_in_g), row(ln_in_b),
                  row(attn_out_g[0]), w_mix_out[0].astype(BF16), row(ln1_g[0]), row(ln1_b[0]),
                  tm=PERM_TILE)

    mem2 = mem.reshape(B * MEM_LEN, D)
    kx = _matmul(mem2, w_xk[0].astype(BF16), 512, BF16)
    vx = _matmul(mem2, w_xv[0].astype(BF16), 512, BF16)
    h2 = _xattn(h1, w_xq[0].astype(BF16), kx, vx, w_xo[0].astype(BF16), row(ln2_g[0]), row(ln2_b[0]),
                seq=S, tm=512)

    out = _ffn(h2, w_ffn_gate[0].astype(BF16), w_ffn_up[0].astype(BF16), w_ffn_down[0].astype(BF16),
               row(ln3_g[0]), row(ln3_b[0]), tm=512, tf=512)
    return out.reshape(B, S, D)
```

```python
import functools
import math

import jax
import jax.numpy as jnp
from jax import lax
from jax.experimental import pallas as pl
from jax.experimental.pallas import tpu as pltpu

F32 = jnp.float32
BF16 = jnp.bfloat16

D_MODEL = 2048
HEAD_DIM = 128
HALF = HEAD_DIM // 2
ATTN_HEADS = 8
GMLP_GROUPS = 8
ATTN_WIDTH = ATTN_HEADS * HEAD_DIM
GMLP_WIDTH = GMLP_GROUPS * HEAD_DIM
GMLP_CHUNK = 128
BAND_BLOCK = 128
ROPE_THETA = 10000.0
MEM_LEN = 256
XATTN_HEADS = 4
XATTN_HEAD_DIM = D_MODEL // XATTN_HEADS
DEPTH = 1
DEEPNORM_ALPHA = (2 * DEPTH) ** 0.25
LN_EPS = 1e-5
NEG_BIG = -1e30

PERM_TILE = 512
PERM_R = 16
PERM_I = PERM_TILE // PERM_R

VMEM_LIMIT = 56 * 1024 * 1024


def _layer_norm(x, g, b):
    mu = jnp.mean(x, axis=-1, keepdims=True)
    xc = x - mu
    var = jnp.mean(xc * xc, axis=-1, keepdims=True)
    return xc * lax.rsqrt(var + LN_EPS) * g + b


def _rms_norm(x, g):
    ms = jnp.mean(x * x, axis=-1, keepdims=True)
    return x * lax.rsqrt(ms + LN_EPS) * g


def _gelu(x):
    return 0.5 * x * (1.0 + lax.erf(x * (1.0 / math.sqrt(2.0))))


def _to_strided(src_ref, dst_ref, col0):
    rows = src_ref.shape[0]
    for t in range(rows // PERM_TILE):
        base = t * PERM_TILE
        for r in range(PERM_R):
            blk = src_ref[pl.ds(base + r, PERM_I, stride=PERM_R), :]
            dst_ref[base + r * PERM_I: base + (r + 1) * PERM_I, col0:col0 + HEAD_DIM] = blk.astype(dst_ref.dtype)


def _from_strided(src_ref):
    rows = src_ref.shape[0]
    parts = []
    for t in range(rows // PERM_TILE):
        base = t * PERM_TILE
        for i in range(PERM_I):
            parts.append(src_ref[pl.ds(base + i, PERM_R, stride=PERM_I), :])
    return jnp.concatenate(parts, axis=0)


def _in_proj_kernel(x_ref, pos_ref, invf_ref, lng_ref, lnb_ref, w_ref, sng_ref, snb_ref,
                    ws_ref, bs_ref, gog_ref,
                    qn_ref, kn_ref, vn_ref, qs_ref, ks_ref, vs_ref, gm_ref,
                    hn_s, cos_s, sin_s, ya_s, yb_s, perm_s, u_s):
    i = pl.program_id(0)
    j = pl.program_id(1)
    tm = x_ref.shape[0]
    slot = i % 2

    def layer_norm_x():
        return _layer_norm(x_ref[...], lng_ref[...], lnb_ref[...]).astype(BF16)

    def project():
        return jnp.dot(hn_s[slot], w_ref[...], preferred_element_type=F32)

    def rope_and_store(y, nat_ref, str_ref):
        c = cos_s[...]
        s = sin_s[...]
        for h in range(ATTN_HEADS):
            sl = slice(h * HEAD_DIM, (h + 1) * HEAD_DIM)
            xh = y[:, sl]
            perm_s[h] = xh * c + pltpu.roll(xh, HALF, axis=1) * s
            nat_ref[:, sl] = perm_s[h].astype(BF16)
            _to_strided(perm_s.at[h], str_ref, h * HEAD_DIM)

    @pl.when(jnp.logical_and(i == 0, j == 0))
    def _first_tile():
        hn_s[0] = layer_norm_x()

    @pl.when(j == 0)
    def _u_matmul():
        ya_s[...] = project()
        ang = pos_ref[...].astype(F32) * invf_ref[...]
        lane = lax.broadcasted_iota(jnp.int32, ang.shape, 1)
        sn = jnp.sin(ang)
        cos_s[...] = jnp.cos(ang)
        sin_s[...] = jnp.where(lane < HALF, -sn, sn)

    @pl.when(j == 1)
    def _g_matmul_u_gelu():
        u_s[...] = _gelu(ya_s[...])
        yb_s[...] = project()

    @pl.when(j == 2)
    def _q_matmul_spatial_gate():
        ya_s[...] = project()
        g = _layer_norm(_gelu(yb_s[...]), sng_ref[...], snb_ref[...]).astype(BF16)
        row = lax.broadcasted_iota(jnp.int32, (GMLP_CHUNK, GMLP_CHUNK), 0)
        col = lax.broadcasted_iota(jnp.int32, (GMLP_CHUNK, GMLP_CHUNK), 1)
        causal = row >= col
        for gr in range(GMLP_GROUPS):
            sl = slice(gr * HEAD_DIM, (gr + 1) * HEAD_DIM)
            w = jnp.where(causal, ws_ref[gr], 0.0).astype(BF16)
            bcol = bs_ref[:, gr:gr + 1]
            for c in range(tm // GMLP_CHUNK):
                rs = slice(c * GMLP_CHUNK, (c + 1) * GMLP_CHUNK)
                mixed = jnp.dot(w, g[rs, sl], preferred_element_type=F32) + bcol
                u_s[rs, sl] = u_s[rs, sl] * mixed
        gm_ref[...] = _rms_norm(u_s[...], gog_ref[...]).astype(BF16)

    @pl.when(j == 3)
    def _k_matmul_q_rope():
        yb_s[...] = project()
        rope_and_store(ya_s[...], qn_ref, qs_ref)

    @pl.when(j == 4)
    def _v_matmul_k_rope():
        y = project()
        rope_and_store(yb_s[...], kn_ref, ks_ref)
        hn_s[1 - slot] = layer_norm_x()
        vn_ref[...] = y.astype(BF16)
        for h in range(ATTN_HEADS):
            perm_s[h] = y[:, h * HEAD_DIM:(h + 1) * HEAD_DIM]
            _to_strided(perm_s.at[h], vs_ref, h * HEAD_DIM)


def _in_proj(x2, pos2, invf, ln_g, ln_b, w_in, sgu_g, sgu_b, w_sp, b_sp_t, gm_g, tm):
    T = x2.shape[0]
    ntile = T // tm
    nseg = w_in.shape[1] // ATTN_WIDTH
    assert nseg == 5
    const = lambda i, j: (0, 0)
    row = lambda i, j: (i, 0)
    qkv_shape = jax.ShapeDtypeStruct((T, ATTN_WIDTH), BF16)
    return pl.pallas_call(
        _in_proj_kernel,
        grid=(ntile, nseg),
        in_specs=[
            pl.BlockSpec((tm, D_MODEL), lambda i, j: (jnp.minimum(i + j // (nseg - 1), ntile - 1), 0)),
            pl.BlockSpec((tm, 1), row),
            pl.BlockSpec((1, HEAD_DIM), const),
            pl.BlockSpec((1, D_MODEL), const),
            pl.BlockSpec((1, D_MODEL), const),
            pl.BlockSpec((D_MODEL, ATTN_WIDTH), lambda i, j: (0, (j + 3) % nseg)),
            pl.BlockSpec((1, GMLP_WIDTH), const),
            pl.BlockSpec((1, GMLP_WIDTH), const),
            pl.BlockSpec((GMLP_GROUPS, GMLP_CHUNK, GMLP_CHUNK), lambda i, j: (0, 0, 0)),
            pl.BlockSpec((GMLP_CHUNK, GMLP_GROUPS), const),
            pl.BlockSpec((1, GMLP_WIDTH), const),
        ],
        out_specs=[pl.BlockSpec((tm, ATTN_WIDTH), row)] * 7,
        out_shape=[qkv_shape] * 6 + [jax.ShapeDtypeStruct((T, GMLP_WIDTH), BF16)],
        scratch_shapes=[
            pltpu.VMEM((2, tm, D_MODEL), BF16),
            pltpu.VMEM((tm, HEAD_DIM), F32),
            pltpu.VMEM((tm, HEAD_DIM), F32),
            pltpu.VMEM((tm, ATTN_WIDTH), F32),
            pltpu.VMEM((tm, ATTN_WIDTH), F32),
            pltpu.VMEM((ATTN_HEADS, tm, HEAD_DIM), F32),
            pltpu.VMEM((tm, GMLP_WIDTH), F32),
        ],
        compiler_params=pltpu.CompilerParams(
            dimension_semantics=("arbitrary", "arbitrary"), vmem_limit_bytes=VMEM_LIMIT),
        name="in_proj",
    )(x2, pos2, invf, ln_g, ln_b, w_in, sgu_g, sgu_b, w_sp, b_sp_t, gm_g)


def _band_attn_kernel(q_ref, kp_ref, kc_ref, vp_ref, vc_ref, o_ref, lse_ref, *, interleave, head_major):
    has_prev = pl.program_id(1) > 0
    blk = (BAND_BLOCK, ATTN_WIDTH)
    q = q_ref[...].reshape(blk)
    kp = kp_ref[...].reshape(blk)
    kc = kc_ref[...].reshape(blk)
    vp = vp_ref[...].reshape(blk)
    vc = vc_ref[...].reshape(blk)

    a_q = lax.broadcasted_iota(jnp.int32, (BAND_BLOCK, BAND_BLOCK), 0)
    a_k = lax.broadcasted_iota(jnp.int32, (BAND_BLOCK, BAND_BLOCK), 1)
    if interleave:
        a_q = 4 * (a_q % PERM_I) + a_q // PERM_I
        a_k = 4 * (a_k % PERM_I) + a_k // PERM_I
    mask = jnp.concatenate([jnp.logical_and(a_k >= a_q, has_prev), a_k <= a_q], axis=1)

    scale = HEAD_DIM ** -0.5
    nt = (((1,), (1,)), ((), ()))
    heads = [slice(h * HEAD_DIM, (h + 1) * HEAD_DIM) for h in range(ATTN_HEADS)]
    scores = [lax.dot_general(q[:, sl], jnp.concatenate([kp[:, sl], kc[:, sl]], axis=0), nt,
                              preferred_element_type=F32) for sl in heads]
    probs, stats = [], []
    for s in scores:
        s = jnp.where(mask, s * scale, NEG_BIG)
        m = jnp.max(s, axis=1, keepdims=True)
        p = jnp.exp(s - m)
        stats.append((m, jnp.sum(p, axis=1, keepdims=True)))
        probs.append(p.astype(BF16))
    accs = [jnp.dot(p, jnp.concatenate([vp[:, sl], vc[:, sl]], axis=0), preferred_element_type=F32)
            for p, sl in zip(probs, heads)]
    lane = lax.broadcasted_iota(jnp.int32, (BAND_BLOCK, HEAD_DIM), 1)
    lse_all = jnp.zeros((BAND_BLOCK, HEAD_DIM), F32)
    for h, (acc, (m, l)) in enumerate(zip(accs, stats)):
        out_h = acc / l
        if head_major:
            o_ref[h] = out_h.reshape(o_ref.shape[1:])
        else:
            o_ref[:, heads[h]] = out_h.astype(o_ref.dtype)
        lse_all = jnp.where(lane == h, m + jnp.log(l), lse_all)
    lse_ref[...] = lse_all.reshape(lse_ref.shape)


def _band_attention(q, k, v, batch, seq, dil):
    T = q.shape[0]
    params = pltpu.CompilerParams(dimension_semantics=("arbitrary",) * 3, vmem_limit_bytes=VMEM_LIMIT)
    if dil == 1:
        nb = seq // BAND_BLOCK
        cur = lambda b, jb, z: (b * nb + jb, 0)
        prev = lambda b, jb, z: (b * nb + jnp.maximum(jb - 1, 0), 0)
        blk = (BAND_BLOCK, ATTN_WIDTH)
        return pl.pallas_call(
            functools.partial(_band_attn_kernel, interleave=False, head_major=False),
            grid=(batch, nb, 1),
            in_specs=[pl.BlockSpec(blk, cur), pl.BlockSpec(blk, prev), pl.BlockSpec(blk, cur),
                      pl.BlockSpec(blk, prev), pl.BlockSpec(blk, cur)],
            out_specs=[pl.BlockSpec(blk, cur), pl.BlockSpec((BAND_BLOCK, HEAD_DIM), cur)],
            out_shape=[jax.ShapeDtypeStruct((T, ATTN_WIDTH), BF16),
                       jax.ShapeDtypeStruct((T, HEAD_DIM), F32)],
            compiler_params=params, name="attn_d1",
        )(q, k, k, v, v)
    ntile = seq // PERM_TILE
    if dil == 4:
        view = (batch * ntile, 4, 4, PERM_I)
        cur = lambda b, t, r0: (b * ntile + t, 0, r0, 0)
        prev = lambda b, t, r0: (b * ntile + jnp.maximum(t - 1, 0), 0, r0, 0)
        blk = (None, 4, None, PERM_I)
        grid = (batch, ntile, 4)
    else:
        nsup = ntile // 4
        view = (batch * ntile, PERM_R, PERM_I)
        cur = lambda b, t, r: (b * nsup + t, r, 0)
        prev = lambda b, t, r: (b * nsup + jnp.maximum(t - 1, 0), r, 0)
        blk = (4, None, PERM_I)
        grid = (batch, nsup, PERM_R)
    last = lambda index: (lambda *g: index(*g) + (0,))
    qv, kv, vv = (a.reshape(view + (ATTN_WIDTH,)) for a in (q, k, v))
    bq = blk + (ATTN_WIDTH,)
    o, lse = pl.pallas_call(
        functools.partial(_band_attn_kernel, interleave=(dil == 4), head_major=True),
        grid=grid,
        in_specs=[pl.BlockSpec(bq, last(cur)), pl.BlockSpec(bq, last(prev)), pl.BlockSpec(bq, last(cur)),
                  pl.BlockSpec(bq, last(prev)), pl.BlockSpec(bq, last(cur))],
        out_specs=[pl.BlockSpec((ATTN_HEADS,) + blk + (HEAD_DIM,), lambda *g: (0,) + cur(*g) + (0,)),
                   pl.BlockSpec(blk + (HEAD_DIM,), last(cur))],
        out_shape=[jax.ShapeDtypeStruct((ATTN_HEADS,) + view + (HEAD_DIM,), F32),
                   jax.ShapeDtypeStruct(view + (HEAD_DIM,), F32)],
        compiler_params=params, name="attn_d%d" % dil,
    )(qv, kv, kv, vv, vv)
    return o.reshape(ATTN_HEADS, T, HEAD_DIM), lse.reshape(T, HEAD_DIM)


def _mix_out_kernel(x_ref, o1_ref, o4_ref, o16_ref, l1_ref, l4_ref, l16_ref, gm_ref,
                    lng_ref, lnb_ref, aog_ref, w_ref, g1_ref, b1_ref, h_ref, attn_s):
    l1 = l1_ref[...]
    l4 = _from_strided(l4_ref)
    l16 = _from_strided(l16_ref)
    m = jnp.maximum(jnp.maximum(l1, l4), l16)
    e1 = jnp.exp(l1 - m)
    e4 = jnp.exp(l4 - m)
    e16 = jnp.exp(l16 - m)
    den = e1 + e4 + e16
    w1, w4, w16 = e1 / den, e4 / den, e16 / den
    for h in range(ATTN_HEADS):
        sl = slice(h * HEAD_DIM, (h + 1) * HEAD_DIM)
        attn_s[:, sl] = (w1[:, h:h + 1] * o1_ref[:, sl].astype(F32)
                         + w4[:, h:h + 1] * _from_strided(o4_ref.at[h])
                         + w16[:, h:h + 1] * _from_strided(o16_ref.at[h]))
    attn_n = _rms_norm(attn_s[...], aog_ref[...]).astype(BF16)
    mixed = jnp.dot(attn_n, w_ref[:ATTN_WIDTH, :], preferred_element_type=F32)
    mixed = mixed + jnp.dot(gm_ref[...], w_ref[ATTN_WIDTH:, :], preferred_element_type=F32)
    h0 = _layer_norm(x_ref[...], lng_ref[...], lnb_ref[...])
    h_ref[...] = _layer_norm(DEEPNORM_ALPHA * h0 + mixed, g1_ref[...], b1_ref[...])


def _mix_out(x2, o1, o4, o16, l1, l4, l16, gm_n, ln_g, ln_b, ao_g, w_mix, g1, b1, tm):
    T = x2.shape[0]
    row = lambda i: (i, 0)
    hrow = lambda i: (0, i, 0)
    const = lambda i: (0, 0)
    return pl.pallas_call(
        _mix_out_kernel,
        grid=(T // tm,),
        in_specs=[
            pl.BlockSpec((tm, D_MODEL), row),
            pl.BlockSpec((tm, ATTN_WIDTH), row),
            pl.BlockSpec((ATTN_HEADS, tm, HEAD_DIM), hrow),
            pl.BlockSpec((ATTN_HEADS, tm, HEAD_DIM), hrow),
            pl.BlockSpec((tm, HEAD_DIM), row),
            pl.BlockSpec((tm, HEAD_DIM), row),
            pl.BlockSpec((tm, HEAD_DIM), row),
            pl.BlockSpec((tm, GMLP_WIDTH), row),
            pl.BlockSpec((1, D_MODEL), const),
            pl.BlockSpec((1, D_MODEL), const),
            pl.BlockSpec((1, ATTN_WIDTH), const),
            pl.BlockSpec((ATTN_WIDTH + GMLP_WIDTH, D_MODEL), const),
            pl.BlockSpec((1, D_MODEL), const),
            pl.BlockSpec((1, D_MODEL), const),
        ],
        out_specs=pl.BlockSpec((tm, D_MODEL), row),
        out_shape=jax.ShapeDtypeStruct((T, D_MODEL), F32),
        scratch_shapes=[pltpu.VMEM((tm, ATTN_WIDTH), F32)],
        compiler_params=pltpu.CompilerParams(
            dimension_semantics=("arbitrary",), vmem_limit_bytes=VMEM_LIMIT),
        name="mix_out",
    )(x2, o1, o4, o16, l1, l4, l16, gm_n, ln_g, ln_b, ao_g, w_mix, g1, b1)


def _matmul_kernel(a_ref, w_ref, o_ref):
    o_ref[...] = jnp.dot(a_ref[...].astype(BF16), w_ref[...],
                         preferred_element_type=F32).astype(o_ref.dtype)


def _matmul(a, w, tn, out_dtype):
    M, K = a.shape
    N = w.shape[1]
    return pl.pallas_call(
        _matmul_kernel,
        grid=(N // tn,),
        in_specs=[pl.BlockSpec((M, K), lambda j: (0, 0)), pl.BlockSpec((K, tn), lambda j: (0, j))],
        out_specs=pl.BlockSpec((M, tn), lambda j: (0, j)),
        out_shape=jax.ShapeDtypeStruct((M, N), out_dtype),
        compiler_params=pltpu.CompilerParams(
            dimension_semantics=("arbitrary",), vmem_limit_bytes=VMEM_LIMIT),
        name="mem_proj",
    )(a, w)


def _xattn_kernel(h_ref, wq_ref, k_ref, v_ref, wo_ref, g_ref, b_ref, out_ref, o_s):
    h = h_ref[...]
    q = jnp.dot(h.astype(BF16), wq_ref[...], preferred_element_type=F32).astype(BF16)
    scale = XATTN_HEAD_DIM ** -0.5
    nt = (((1,), (1,)), ((), ()))
    for hd in range(XATTN_HEADS):
        sl = slice(hd * XATTN_HEAD_DIM, (hd + 1) * XATTN_HEAD_DIM)
        s = lax.dot_general(q[:, sl], k_ref[:, sl], nt, preferred_element_type=F32) * scale
        m = jnp.max(s, axis=1, keepdims=True)
        p = jnp.exp(s - m)
        l = jnp.sum(p, axis=1, keepdims=True)
        o = jnp.dot(p.astype(BF16), v_ref[:, sl], preferred_element_type=F32) / l
        o_s[:, sl] = o.astype(BF16)
    xa = jnp.dot(o_s[...], wo_ref[...], preferred_element_type=F32)
    out_ref[...] = _layer_norm(DEEPNORM_ALPHA * h + xa, g_ref[...], b_ref[...])


def _xattn(h1, wq, kx, vx, wo, g2, b2, seq, tm):
    T = h1.shape[0]
    per_batch = seq // tm
    row = lambda i: (i, 0)
    const = lambda i: (0, 0)
    mem = lambda i: (i // per_batch, 0)
    return pl.pallas_call(
        _xattn_kernel,
        grid=(T // tm,),
        in_specs=[
            pl.BlockSpec((tm, D_MODEL), row),
            pl.BlockSpec((D_MODEL, D_MODEL), const),
            pl.BlockSpec((MEM_LEN, D_MODEL), mem),
            pl.BlockSpec((MEM_LEN, D_MODEL), mem),
            pl.BlockSpec((D_MODEL, D_MODEL), const),
            pl.BlockSpec((1, D_MODEL), const),
            pl.BlockSpec((1, D_MODEL), const),
        ],
        out_specs=pl.BlockSpec((tm, D_MODEL), row),
        out_shape=jax.ShapeDtypeStruct((T, D_MODEL), F32),
        scratch_shapes=[pltpu.VMEM((tm, D_MODEL), BF16)],
        compiler_params=pltpu.CompilerParams(
            dimension_semantics=("arbitrary",), vmem_limit_bytes=VMEM_LIMIT),
        name="xattn",
    )(h1, wq, kx, vx, wo, g2, b2)


def _ffn_kernel(h_ref, wg_ref, wu_ref, wd_ref, g_ref, b_ref, out_ref, hb_s):
    f = pl.program_id(1)

    @pl.when(f == 0)
    def _init():
        hb_s[...] = h_ref[...].astype(BF16)
        out_ref[...] = jnp.zeros_like(out_ref)

    hb = hb_s[...]
    gate = jnp.dot(hb, wg_ref[...], preferred_element_type=F32)
    up = jnp.dot(hb, wu_ref[...], preferred_element_type=F32)
    act = (gate * (1.0 / (1.0 + jnp.exp(-gate))) * up).astype(BF16)
    out_ref[...] += jnp.dot(act, wd_ref[...], preferred_element_type=F32)

    @pl.when(f == pl.num_programs(1) - 1)
    def _finish():
        out_ref[...] = _layer_norm(DEEPNORM_ALPHA * h_ref[...] + out_ref[...], g_ref[...], b_ref[...])


def _ffn(h2, wg, wu, wd, g3, b3, tm, tf):
    T = h2.shape[0]
    d_ff = wg.shape[1]
    return pl.pallas_call(
        _ffn_kernel,
        grid=(T // tm, d_ff // tf),
        in_specs=[
            pl.BlockSpec((tm, D_MODEL), lambda i, f: (i, 0)),
            pl.BlockSpec((D_MODEL, tf), lambda i, f: (0, f)),
            pl.BlockSpec((D_MODEL, tf), lambda i, f: (0, f)),
            pl.BlockSpec((tf, D_MODEL), lambda i, f: (f, 0)),
            pl.BlockSpec((1, D_MODEL), lambda i, f: (0, 0)),
            pl.BlockSpec((1, D_MODEL), lambda i, f: (0, 0)),
        ],
        out_specs=pl.BlockSpec((tm, D_MODEL), lambda i, f: (i, 0)),
        out_shape=jax.ShapeDtypeStruct((T, D_MODEL), F32),
        scratch_shapes=[pltpu.VMEM((tm, D_MODEL), BF16)],
        compiler_params=pltpu.CompilerParams(
            dimension_semantics=("arbitrary", "arbitrary"), vmem_limit_bytes=VMEM_LIMIT),
        name="ffn",
    )(h2, wg, wu, wd, g3, b3)


def kernel(x, mem, positions, ln_in_g, ln_in_b, w_in, sgu_norm_g, sgu_norm_b, w_spatial, b_spatial,
           attn_out_g, gmlp_out_g, w_mix_out, ln1_g, ln1_b, w_xq, w_xk, w_xv, w_xo, ln2_g, ln2_b,
           w_ffn_gate, w_ffn_up, w_ffn_down, ln3_g, ln3_b):
    B, S, D = x.shape
    T = B * S
    assert D == D_MODEL and S % (4 * PERM_TILE) == 0 and w_in.shape[0] == DEPTH == 1
    row = lambda v: v.reshape(1, -1)

    x2 = x.reshape(T, D)
    pos2 = positions.reshape(T, 1)
    inv_freq = ROPE_THETA ** (-jnp.arange(HALF, dtype=F32) / HALF)
    invf = jnp.concatenate([inv_freq, inv_freq]).reshape(1, HEAD_DIM)

    qn, kn, vn, qs, ks, vs, gm_n = _in_proj(
        x2, pos2, invf, row(ln_in_g), row(ln_in_b), w_in[0].astype(BF16),
        row(sgu_norm_g[0]), row(sgu_norm_b[0]), w_spatial[0], b_spatial[0].T, row(gmlp_out_g[0]),
        tm=PERM_TILE)

    o1, l1 = _band_attention(qn, kn, vn, B, S, 1)
    o4, l4 = _band_attention(qs, ks, vs, B, S, 4)
    o16, l16 = _band_attention(qs, ks, vs, B, S, 16)

    h1 = _mix_out(x2, o1, o4, o16, l1, l4, l16, gm_n, row(ln_in_g), row(ln_in_b),
                  row(attn_out_g[0]), w_mix_out[0].astype(BF16), row(ln1_g[0]), row(ln1_b[0]),
                  tm=PERM_TILE)

    mem2 = mem.reshape(B * MEM_LEN, D)
    kx = _matmul(mem2, w_xk[0].astype(BF16), 512, BF16)
    vx = _matmul(mem2, w_xv[0].astype(BF16), 512, BF16)
    h2 = _xattn(h1, w_xq[0].astype(BF16), kx, vx, w_xo[0].astype(BF16), row(ln2_g[0]), row(ln2_b[0]),
                seq=S, tm=512)

    out = _ffn(h2, w_ffn_gate[0].astype(BF16), w_ffn_up[0].astype(BF16), w_ffn_down[0].astype(BF16),
               row(ln3_g[0]), row(ln3_b[0]), tm=512, tf=512)
    return out.reshape(B, S, D)
```

```python
import math

import jax
import jax.numpy as jnp
from jax import lax
from jax.experimental import pallas as pl
from jax.experimental.pallas import tpu as pltpu

F32 = jnp.float32
BF16 = jnp.bfloat16

D_MODEL = 2048
HEAD_DIM = 128
HALF = HEAD_DIM // 2
ATTN_HEADS = 8
GMLP_GROUPS = 8
ATTN_WIDTH = ATTN_HEADS * HEAD_DIM
GMLP_WIDTH = GMLP_GROUPS * HEAD_DIM
GMLP_CHUNK = 128
BAND_BLOCK = 128
ROPE_THETA = 10000.0
MEM_LEN = 256
XATTN_HEADS = 4
XATTN_HEAD_DIM = D_MODEL // XATTN_HEADS
DEPTH = 1
DEEPNORM_ALPHA = (2 * DEPTH) ** 0.25
LN_EPS = 1e-5
NEG_BIG = -1e30

PERM_TILE = 512
PERM_R = 16
PERM_I = PERM_TILE // PERM_R

VMEM_LIMIT = 60 * 1024 * 1024

ATT_SUPER = 4 * PERM_TILE
ATT_HEADS_PER_STEP = 2
ATT_GROUP = 4
ATT_MERGE_ROWS = 256
FFN_DOWN_COLS = 1024


def _layer_norm(x, g, b):
    mu = jnp.mean(x, axis=-1, keepdims=True)
    xc = x - mu
    var = jnp.mean(xc * xc, axis=-1, keepdims=True)
    return xc * lax.rsqrt(var + LN_EPS) * g + b


def _rms_norm(x, g):
    ms = jnp.mean(x * x, axis=-1, keepdims=True)
    return x * lax.rsqrt(ms + LN_EPS) * g


def _gelu(x):
    return 0.5 * x * (1.0 + lax.erf(x * (1.0 / math.sqrt(2.0))))


def _to_strided(src_ref, dst_ref, col0):
    rows = src_ref.shape[0]
    for t in range(rows // PERM_TILE):
        base = t * PERM_TILE
        for r in range(PERM_R):
            blk = src_ref[pl.ds(base + r, PERM_I, stride=PERM_R), :]
            dst_ref[base + r * PERM_I: base + (r + 1) * PERM_I, col0:col0 + HEAD_DIM] = blk.astype(dst_ref.dtype)


def _in_proj_kernel(x_ref, pos_ref, invf_ref, lng_ref, lnb_ref, w_ref, sng_ref, snb_ref,
                    ws_ref, bs_ref, gog_ref,
                    qn_ref, kn_ref, vn_ref, qs_ref, ks_ref, vs_ref, gm_ref,
                    hn_s, cos_s, sin_s, ya_s, yb_s, u_s):
    i = pl.program_id(0)
    j = pl.program_id(1)
    tm = x_ref.shape[0]
    head = lambda h: slice(h * HEAD_DIM, (h + 1) * HEAD_DIM)

    def layer_norm_x():
        hn_s[...] = _layer_norm(x_ref[...], lng_ref[...], lnb_ref[...]).astype(BF16)

    def project_into(y_s, seg):
        y = jnp.dot(hn_s[...], w_ref[:, seg * ATTN_WIDTH:(seg + 1) * ATTN_WIDTH],
                    preferred_element_type=F32)
        for h in range(ATTN_HEADS):
            y_s[h] = y[:, head(h)]

    def rope_and_store(y_s, nat_ref, str_ref):
        c = cos_s[...]
        s = sin_s[...]
        for h in range(ATTN_HEADS):
            xh = y_s[h]
            y_s[h] = xh * c + pltpu.roll(xh, HALF, axis=1) * s
            nat_ref[:, head(h)] = y_s[h].astype(BF16)
            _to_strided(y_s.at[h], str_ref, h * HEAD_DIM)

    @pl.when(jnp.logical_and(i == 0, j == 0))
    def _first_tile():
        layer_norm_x()

    @pl.when(j == 0)
    def _u_matmul():
        project_into(ya_s, 3)
        ang = pos_ref[...].astype(F32) * invf_ref[...]
        lane = lax.broadcasted_iota(jnp.int32, ang.shape, 1)
        sn = jnp.sin(ang)
        cos_s[...] = jnp.cos(ang)
        sin_s[...] = jnp.where(lane < HALF, -sn, sn)

    @pl.when(j == 1)
    def _g_matmul_u_gelu():
        for h in range(GMLP_GROUPS):
            u_s[:, head(h)] = _gelu(ya_s[h])
        project_into(yb_s, 4)

    @pl.when(j == 2)
    def _q_matmul_spatial_gate():
        project_into(ya_s, 0)
        g = jnp.concatenate([_gelu(yb_s[h]) for h in range(GMLP_GROUPS)], axis=1)
        g = _layer_norm(g, sng_ref[...], snb_ref[...]).astype(BF16)
        row = lax.broadcasted_iota(jnp.int32, (GMLP_CHUNK, GMLP_CHUNK), 0)
        col = lax.broadcasted_iota(jnp.int32, (GMLP_CHUNK, GMLP_CHUNK), 1)
        causal = row >= col
        for gr in range(GMLP_GROUPS):
            w = jnp.where(causal, ws_ref[gr], 0.0).astype(BF16)
            bcol = bs_ref[:, gr:gr + 1]
            for c in range(tm // GMLP_CHUNK):
                rs = slice(c * GMLP_CHUNK, (c + 1) * GMLP_CHUNK)
                mixed = jnp.dot(w, g[rs, head(gr)], preferred_element_type=F32) + bcol
                u_s[rs, head(gr)] = u_s[rs, head(gr)] * mixed
        gm_ref[...] = _rms_norm(u_s[...], gog_ref[...]).astype(BF16)

    @pl.when(j == 3)
    def _k_matmul_q_rope():
        project_into(yb_s, 1)
        rope_and_store(ya_s, qn_ref, qs_ref)

    @pl.when(j == 4)
    def _v_matmul_k_rope():
        project_into(ya_s, 2)
        rope_and_store(yb_s, kn_ref, ks_ref)
        layer_norm_x()
        for h in range(ATTN_HEADS):
            vn_ref[:, head(h)] = ya_s[h].astype(BF16)
            _to_strided(ya_s.at[h], vs_ref, h * HEAD_DIM)


def _in_proj(x2, pos2, invf, ln_g, ln_b, w_in, sgu_g, sgu_b, w_sp, b_sp_t, gm_g, tm):
    T = x2.shape[0]
    ntile = T // tm
    nseg = w_in.shape[1] // ATTN_WIDTH
    assert nseg == 5
    const = lambda i, j: (0, 0)
    row = lambda i, j: (i, 0)
    qkv_shape = jax.ShapeDtypeStruct((T, ATTN_WIDTH), BF16)
    return pl.pallas_call(
        _in_proj_kernel,
        grid=(ntile, nseg),
        in_specs=[
            pl.BlockSpec((tm, D_MODEL), lambda i, j: (jnp.minimum(i + j // (nseg - 1), ntile - 1), 0)),
            pl.BlockSpec((tm, 1), row),
            pl.BlockSpec((1, HEAD_DIM), const),
            pl.BlockSpec((1, D_MODEL), const),
            pl.BlockSpec((1, D_MODEL), const),
            pl.BlockSpec((D_MODEL, nseg * ATTN_WIDTH), const, pipeline_mode=pl.Buffered(1)),
            pl.BlockSpec((1, GMLP_WIDTH), const),
            pl.BlockSpec((1, GMLP_WIDTH), const),
            pl.BlockSpec((GMLP_GROUPS, GMLP_CHUNK, GMLP_CHUNK), lambda i, j: (0, 0, 0)),
            pl.BlockSpec((GMLP_CHUNK, GMLP_GROUPS), const),
            pl.BlockSpec((1, GMLP_WIDTH), const),
        ],
        out_specs=[pl.BlockSpec((tm, ATTN_WIDTH), row)] * 7,
        out_shape=[qkv_shape] * 6 + [jax.ShapeDtypeStruct((T, GMLP_WIDTH), BF16)],
        scratch_shapes=[
            pltpu.VMEM((tm, D_MODEL), BF16),
            pltpu.VMEM((tm, HEAD_DIM), F32),
            pltpu.VMEM((tm, HEAD_DIM), F32),
            pltpu.VMEM((ATTN_HEADS, tm, HEAD_DIM), F32),
            pltpu.VMEM((ATTN_HEADS, tm, HEAD_DIM), F32),
            pltpu.VMEM((tm, GMLP_WIDTH), F32),
        ],
        compiler_params=pltpu.CompilerParams(
            dimension_semantics=("arbitrary", "arbitrary"), vmem_limit_bytes=VMEM_LIMIT),
        name="in_proj",
    )(x2, pos2, invf, ln_g, ln_b, w_in, sgu_g, sgu_b, w_sp, b_sp_t, gm_g)


def _dilated_attn_kernel(qn_ref, knp_ref, knc_ref, vnp_ref, vnc_ref,
                         qs_ref, ksp_ref, ksc_ref, vsp_ref, vsc_ref, out_ref,
                         o1_s, o4_s, o16_s, l1_s, l4_s, l16_s):
    has_prev = pl.program_id(1) > 0
    nheads = out_ref.shape[1] // HEAD_DIM
    scale = HEAD_DIM ** -0.5
    nt = (((1,), (1,)), ((), ()))
    head = lambda h: slice(h * HEAD_DIM, (h + 1) * HEAD_DIM)
    lane = lax.broadcasted_iota(jnp.int32, (BAND_BLOCK, HEAD_DIM), 1)
    a_q = lax.broadcasted_iota(jnp.int32, (BAND_BLOCK, BAND_BLOCK), 0)
    a_k = lax.broadcasted_iota(jnp.int32, (BAND_BLOCK, BAND_BLOCK), 1)

    def band_masks(order):
        iq, ik = order(a_q), order(a_k)
        first = jnp.concatenate([jnp.logical_and(ik >= iq, has_prev), ik <= iq], axis=1)
        rest = jnp.concatenate([ik >= iq, ik <= iq], axis=1)
        return first, rest

    nat_first, nat_rest = band_masks(lambda a: a)
    il_first, il_rest = band_masks(lambda a: 4 * (a % PERM_I) + a // PERM_I)

    def gather(ref, runs, h):
        parts = [ref[start:start + size, head(h)] for start, size in runs]
        return parts[0] if len(parts) == 1 else jnp.concatenate(parts, axis=0)

    def store_natural(o_s, l_s, blk):
        def store(h, out, lse_blk):
            r0 = blk * BAND_BLOCK
            o_s[h, r0:r0 + BAND_BLOCK, :] = out
            if lse_blk is not None:
                l_s[r0:r0 + BAND_BLOCK, :] = lse_blk
        return store

    def store_strided(o_s, l_s, targets):
        def store(h, out, lse_blk):
            at = 0
            for first_row, size in targets:
                o_s[h, pl.ds(first_row, size, stride=PERM_R), :] = out[at:at + size]
                if lse_blk is not None:
                    l_s[pl.ds(first_row, size, stride=PERM_R), :] = lse_blk[at:at + size]
                at += size
        return store

    blocks = []
    for blk in range(ATT_SUPER // BAND_BLOCK):
        cur = [(blk * BAND_BLOCK, BAND_BLOCK)]
        if blk == 0:
            prev, mask = (knp_ref, vnp_ref, [(0, BAND_BLOCK)]), nat_first
        else:
            prev, mask = (knc_ref, vnc_ref, [((blk - 1) * BAND_BLOCK, BAND_BLOCK)]), nat_rest
        blocks.append((qn_ref, cur, prev, (knc_ref, vnc_ref, cur), mask, store_natural(o1_s, l1_s, blk)))
    for t in range(ATT_SUPER // PERM_TILE):
        for r0 in range(4):
            runs_of = lambda tile: [(tile * PERM_TILE + (4 * r1 + r0) * PERM_I, PERM_I) for r1 in range(4)]
            if t == 0:
                prev, mask = (ksp_ref, vsp_ref, runs_of(ATT_SUPER // PERM_TILE - 1)), il_first
            else:
                prev, mask = (ksc_ref, vsc_ref, runs_of(t - 1)), il_rest
            targets = [(t * PERM_TILE + 4 * r1 + r0, PERM_I) for r1 in range(4)]
            blocks.append((qs_ref, runs_of(t), prev, (ksc_ref, vsc_ref, runs_of(t)), mask,
                           store_strided(o4_s, l4_s, targets)))
    for r in range(PERM_R):
        runs = [(k * PERM_TILE + r * PERM_I, PERM_I) for k in range(ATT_SUPER // PERM_TILE)]
        blocks.append((qs_ref, runs, (ksp_ref, vsp_ref, runs), (ksc_ref, vsc_ref, runs), nat_first,
                       store_strided(o16_s, l16_s, [(r, BAND_BLOCK)])))

    for g0 in range(0, len(blocks), ATT_GROUP):
        group = blocks[g0:g0 + ATT_GROUP]
        units = [(bi, h) for bi in range(len(group)) for h in range(nheads)]
        scores = []
        for bi, h in units:
            q_ref, q_runs, (kp_ref, _, p_runs), (kc_ref, _, c_runs), _, _ = group[bi]
            keys = jnp.concatenate([gather(kp_ref, p_runs, h), gather(kc_ref, c_runs, h)], axis=0)
            scores.append(lax.dot_general(gather(q_ref, q_runs, h), keys, nt, preferred_element_type=F32))
        probs, stats = [], []
        for (bi, h), s in zip(units, scores):
            s = jnp.where(group[bi][4], s * scale, NEG_BIG)
            m = jnp.max(s, axis=1, keepdims=True)
            p = jnp.exp(s - m)
            stats.append((m, jnp.sum(p, axis=1, keepdims=True)))
            probs.append(p.astype(BF16))
        lse_blk = [jnp.zeros((BAND_BLOCK, HEAD_DIM), F32) for _ in group]
        for (bi, h), p, (m, l) in zip(units, probs, stats):
            _, _, (_, vp_ref, p_runs), (_, vc_ref, c_runs), _, store = group[bi]
            vals = jnp.concatenate([gather(vp_ref, p_runs, h), gather(vc_ref, c_runs, h)], axis=0)
            out = jnp.dot(p, vals, preferred_element_type=F32) / l
            lse_blk[bi] = jnp.where(lane == h, m + jnp.log(l), lse_blk[bi])
            store(h, out, lse_blk[bi] if h == nheads - 1 else None)

    for r0 in range(0, ATT_SUPER, ATT_MERGE_ROWS):
        rs = slice(r0, r0 + ATT_MERGE_ROWS)
        l1, l4, l16 = l1_s[rs, :], l4_s[rs, :], l16_s[rs, :]
        m = jnp.maximum(jnp.maximum(l1, l4), l16)
        e1, e4, e16 = jnp.exp(l1 - m), jnp.exp(l4 - m), jnp.exp(l16 - m)
        den = e1 + e4 + e16
        w1, w4, w16 = e1 / den, e4 / den, e16 / den
        for h in range(nheads):
            merged = (w1[:, h:h + 1] * o1_s[h, rs, :] + w4[:, h:h + 1] * o4_s[h, rs, :]
                      + w16[:, h:h + 1] * o16_s[h, rs, :])
            out_ref[rs, head(h)] = merged.astype(out_ref.dtype)


def _dilated_attention(qn, kn, vn, qs, ks, vs, batch, seq):
    T = qn.shape[0]
    nsup = seq // ATT_SUPER
    per_sup = ATT_SUPER // BAND_BLOCK
    width = ATT_HEADS_PER_STEP * HEAD_DIM
    cur = lambda b, c, g: (b * nsup + c, g)
    prev_sup = lambda b, c, g: (b * nsup + jnp.maximum(c - 1, 0), g)
    prev_blk = lambda b, c, g: ((b * nsup + c) * per_sup - jnp.minimum(c, 1), g)
    sup = pl.BlockSpec((ATT_SUPER, width), cur)
    sup_prev = pl.BlockSpec((ATT_SUPER, width), prev_sup)
    blk_prev = pl.BlockSpec((BAND_BLOCK, width), prev_blk)
    head_major = pltpu.VMEM((ATT_HEADS_PER_STEP, ATT_SUPER, HEAD_DIM), F32)
    lse = pltpu.VMEM((ATT_SUPER, HEAD_DIM), F32)
    return pl.pallas_call(
        _dilated_attn_kernel,
        grid=(batch, nsup, ATTN_HEADS // ATT_HEADS_PER_STEP),
        in_specs=[sup, blk_prev, sup, blk_prev, sup, sup, sup_prev, sup, sup_prev, sup],
        out_specs=sup,
        out_shape=jax.ShapeDtypeStruct((T, ATTN_WIDTH), BF16),
        scratch_shapes=[head_major, head_major, head_major, lse, lse, lse],
        compiler_params=pltpu.CompilerParams(
            dimension_semantics=("arbitrary",) * 3, vmem_limit_bytes=VMEM_LIMIT),
        name="dilated_attn",
    )(qn, kn, kn, vn, vn, qs, ks, ks, vs, vs)


def _mix_out_kernel(x_ref, attn_ref, gm_ref, lng_ref, lnb_ref, aog_ref, w_ref, g1_ref, b1_ref, h_ref):
    attn_n = _rms_norm(attn_ref[...].astype(F32), aog_ref[...]).astype(BF16)
    mixed = jnp.dot(attn_n, w_ref[:ATTN_WIDTH, :], preferred_element_type=F32)
    mixed = mixed + jnp.dot(gm_ref[...], w_ref[ATTN_WIDTH:, :], preferred_element_type=F32)
    h0 = _layer_norm(x_ref[...], lng_ref[...], lnb_ref[...])
    h_ref[...] = _layer_norm(DEEPNORM_ALPHA * h0 + mixed, g1_ref[...], b1_ref[...])


def _mix_out(x2, attn, gm_n, ln_g, ln_b, ao_g, w_mix, g1, b1, tm):
    T = x2.shape[0]
    row = lambda i: (i, 0)
    const = lambda i: (0, 0)
    return pl.pallas_call(
        _mix_out_kernel,
        grid=(T // tm,),
        in_specs=[
            pl.BlockSpec((tm, D_MODEL), row),
            pl.BlockSpec((tm, ATTN_WIDTH), row),
            pl.BlockSpec((tm, GMLP_WIDTH), row),
            pl.BlockSpec((1, D_MODEL), const),
            pl.BlockSpec((1, D_MODEL), const),
            pl.BlockSpec((1, ATTN_WIDTH), const),
            pl.BlockSpec((ATTN_WIDTH + GMLP_WIDTH, D_MODEL), const),
            pl.BlockSpec((1, D_MODEL), const),
            pl.BlockSpec((1, D_MODEL), const),
        ],
        out_specs=pl.BlockSpec((tm, D_MODEL), row),
        out_shape=jax.ShapeDtypeStruct((T, D_MODEL), F32),
        compiler_params=pltpu.CompilerParams(
            dimension_semantics=("arbitrary",), vmem_limit_bytes=VMEM_LIMIT),
        name="mix_out",
    )(x2, attn, gm_n, ln_g, ln_b, ao_g, w_mix, g1, b1)


def _matmul_kernel(a_ref, w_ref, o_ref):
    o_ref[...] = jnp.dot(a_ref[...].astype(BF16), w_ref[...],
                         preferred_element_type=F32).astype(o_ref.dtype)


def _matmul(a, w, tn, out_dtype):
    M, K = a.shape
    N = w.shape[1]
    return pl.pallas_call(
        _matmul_kernel,
        grid=(N // tn,),
        in_specs=[pl.BlockSpec((M, K), lambda j: (0, 0)), pl.BlockSpec((K, tn), lambda j: (0, j))],
        out_specs=pl.BlockSpec((M, tn), lambda j: (0, j)),
        out_shape=jax.ShapeDtypeStruct((M, N), out_dtype),
        compiler_params=pltpu.CompilerParams(
            dimension_semantics=("arbitrary",), vmem_limit_bytes=VMEM_LIMIT),
        name="mem_proj",
    )(a, w)


def _xattn_kernel(h_ref, wq_ref, k_ref, v_ref, wo_ref, g_ref, b_ref, out_ref, o_s):
    h = h_ref[...]
    q = jnp.dot(h.astype(BF16), wq_ref[...], preferred_element_type=F32).astype(BF16)
    scale = XATTN_HEAD_DIM ** -0.5
    nt = (((1,), (1,)), ((), ()))
    for hd in range(XATTN_HEADS):
        sl = slice(hd * XATTN_HEAD_DIM, (hd + 1) * XATTN_HEAD_DIM)
        s = lax.dot_general(q[:, sl], k_ref[:, sl], nt, preferred_element_type=F32) * scale
        m = jnp.max(s, axis=1, keepdims=True)
        p = jnp.exp(s - m)
        l = jnp.sum(p, axis=1, keepdims=True)
        o = jnp.dot(p.astype(BF16), v_ref[:, sl], preferred_element_type=F32) / l
        o_s[:, sl] = o.astype(BF16)
    xa = jnp.dot(o_s[...], wo_ref[...], preferred_element_type=F32)
    out_ref[...] = _layer_norm(DEEPNORM_ALPHA * h + xa, g_ref[...], b_ref[...])


def _xattn(h1, wq, kx, vx, wo, g2, b2, seq, tm):
    T = h1.shape[0]
    per_batch = seq // tm
    row = lambda i: (i, 0)
    const = lambda i: (0, 0)
    mem = lambda i: (i // per_batch, 0)
    return pl.pallas_call(
        _xattn_kernel,
        grid=(T // tm,),
        in_specs=[
            pl.BlockSpec((tm, D_MODEL), row),
            pl.BlockSpec((D_MODEL, D_MODEL), const),
            pl.BlockSpec((MEM_LEN, D_MODEL), mem),
            pl.BlockSpec((MEM_LEN, D_MODEL), mem),
            pl.BlockSpec((D_MODEL, D_MODEL), const),
            pl.BlockSpec((1, D_MODEL), const),
            pl.BlockSpec((1, D_MODEL), const),
        ],
        out_specs=pl.BlockSpec((tm, D_MODEL), row),
        out_shape=jax.ShapeDtypeStruct((T, D_MODEL), F32),
        scratch_shapes=[pltpu.VMEM((tm, D_MODEL), BF16)],
        compiler_params=pltpu.CompilerParams(
            dimension_semantics=("arbitrary",), vmem_limit_bytes=VMEM_LIMIT),
        name="xattn",
    )(h1, wq, kx, vx, wo, g2, b2)


def _ffn_kernel(h_ref, wg_ref, wu_ref, wd_ref, g_ref, b_ref, out_ref, hb_s):
    f = pl.program_id(1)

    @pl.when(f == 0)
    def _init():
        hb_s[...] = h_ref[...].astype(BF16)
        out_ref[...] = jnp.zeros_like(out_ref)

    hb = hb_s[...]
    gate = jnp.dot(hb, wg_ref[...], preferred_element_type=F32)
    up = jnp.dot(hb, wu_ref[...], preferred_element_type=F32)
    act = (gate * (1.0 / (1.0 + jnp.exp(-gate))) * up).astype(BF16)
    for c0 in range(0, D_MODEL, FFN_DOWN_COLS):
        cs = slice(c0, c0 + FFN_DOWN_COLS)
        out_ref[:, cs] += jnp.dot(act, wd_ref[:, cs], preferred_element_type=F32)

    @pl.when(f == pl.num_programs(1) - 1)
    def _finish():
        out_ref[...] = _layer_norm(DEEPNORM_ALPHA * h_ref[...] + out_ref[...], g_ref[...], b_ref[...])


def _ffn(h2, wg, wu, wd, g3, b3, tm, tf):
    T = h2.shape[0]
    d_ff = wg.shape[1]
    return pl.pallas_call(
        _ffn_kernel,
        grid=(T // tm, d_ff // tf),
        in_specs=[
            pl.BlockSpec((tm, D_MODEL), lambda i, f: (i, 0)),
            pl.BlockSpec((D_MODEL, tf), lambda i, f: (0, f)),
            pl.BlockSpec((D_MODEL, tf), lambda i, f: (0, f)),
            pl.BlockSpec((tf, D_MODEL), lambda i, f: (f, 0)),
            pl.BlockSpec((1, D_MODEL), lambda i, f: (0, 0)),
            pl.BlockSpec((1, D_MODEL), lambda i, f: (0, 0)),
        ],
        out_specs=pl.BlockSpec((tm, D_MODEL), lambda i, f: (i, 0)),
        out_shape=jax.ShapeDtypeStruct((T, D_MODEL), F32),
        scratch_shapes=[pltpu.VMEM((tm, D_MODEL), BF16)],
        compiler_params=pltpu.CompilerParams(
            dimension_semantics=("arbitrary", "arbitrary"), vmem_limit_bytes=VMEM_LIMIT),
        name="ffn",
    )(h2, wg, wu, wd, g3, b3)


def kernel(x, mem, positions, ln_in_g, ln_in_b, w_in, sgu_norm_g, sgu_norm_b, w_spatial, b_spatial,
           attn_out_g, gmlp_out_g, w_mix_out, ln1_g, ln1_b, w_xq, w_xk, w_xv, w_xo, ln2_g, ln2_b,
           w_ffn_gate, w_ffn_up, w_ffn_down, ln3_g, ln3_b):
    B, S, D = x.shape
    T = B * S
    assert D == D_MODEL and S % ATT_SUPER == 0 and w_in.shape[0] == DEPTH == 1
    row = lambda v: v.reshape(1, -1)

    x2 = x.reshape(T, D)
    pos2 = positions.reshape(T, 1)
    inv_freq = ROPE_THETA ** (-jnp.arange(HALF, dtype=F32) / HALF)
    invf = jnp.concatenate([inv_freq, inv_freq]).reshape(1, HEAD_DIM)

    qn, kn, vn, qs, ks, vs, gm_n = _in_proj(
        x2, pos2, invf, row(ln_in_g), row(ln_in_b), w_in[0].astype(BF16),
        row(sgu_norm_g[0]), row(sgu_norm_b[0]), w_spatial[0], b_spatial[0].T, row(gmlp_out_g[0]),
        tm=PERM_TILE)

    attn = _dilated_attention(qn, kn, vn, qs, ks, vs, B, S)

    h1 = _mix_out(x2, attn, gm_n, row(ln_in_g), row(ln_in_b),
                  row(attn_out_g[0]), w_mix_out[0].astype(BF16), row(ln1_g[0]), row(ln1_b[0]),
                  tm=PERM_TILE)

    mem2 = mem.reshape(B * MEM_LEN, D)
    kx = _matmul(mem2, w_xk[0].astype(BF16), 512, BF16)
    vx = _matmul(mem2, w_xv[0].astype(BF16), 512, BF16)
    h2 = _xattn(h1, w_xq[0].astype(BF16), kx, vx, w_xo[0].astype(BF16), row(ln2_g[0]), row(ln2_b[0]),
                seq=S, tm=512)

    out = _ffn(h2, w_ffn_gate[0].astype(BF16), w_ffn_up[0].astype(BF16), w_ffn_down[0].astype(BF16),
               row(ln3_g[0]), row(ln3_b[0]), tm=512, tf=512)
    return out.reshape(B, S, D)
```
